```python
import math
import jax, jax.numpy as jnp
from jax import lax
import numpy as np

D_MODEL = 1024
BATCH = 4
SEQ = 8192
DEPTH = 2

N_EVEN = (DEPTH + 1) // 2
N_ODD = DEPTH // 2

HG_HEADS = 4
HG_DK = 128
HG_DV = 128
HG_WIDTH = HG_HEADS * HG_DK
HG_CHUNK = 32

CV_CH = D_MODEL - HG_HEADS * HG_DV
CV_K = 31

IN_COLS = 4 * HG_WIDTH + 2 * CV_CH

POOL_WINDOWS = (2, 4, 8, 16)
POOL_GROUPS = len(POOL_WINDOWS)
POOL_CH = D_MODEL // POOL_GROUPS

MEM_LEN = 256
XA_HEADS = 4
XA_DH = D_MODEL // XA_HEADS

D_FF = -(-(8 * D_MODEL) // (3 * 256)) * 256

ALPHA = (2 * DEPTH) ** 0.25
BETA = (8 * DEPTH) ** -0.25
LN_EPS = 1e-5
RMS_EPS = 1e-6

kernel_name = "hgrn2_conformer_pool_hybrid_deepnorm"


def layer_norm(x, g, b):
    xf = x.astype(jnp.float32)
    mu = jnp.mean(xf, axis=-1, keepdims=True)
    var = jnp.mean(jnp.square(xf - mu), axis=-1, keepdims=True)
    y = (xf - mu) * lax.rsqrt(var + LN_EPS) * g.astype(jnp.float32) + b.astype(jnp.float32)
    return y.astype(x.dtype)


def hgrn2_chunked(q, k, v, log_f):
    B, S, H, DK = q.shape
    DV = v.shape[-1]
    nc = S // HG_CHUNK

    def to_chunks(t):
        return t.astype(jnp.float32).reshape(B, nc, HG_CHUNK, H, t.shape[-1]).transpose(0, 3, 1, 2, 4)

    q, k, v, log_f = to_chunks(q), to_chunks(k), to_chunks(v), to_chunks(log_f)
    b = jnp.cumsum(log_f, axis=3)
    b_last = b[:, :, :, -1:, :]
    q_t = q * jnp.exp(b)
    k_t = k * jnp.exp(-b)
    causal = jnp.tril(jnp.ones((HG_CHUNK, HG_CHUNK), dtype=bool))
    scores = jnp.einsum('bhntd,bhnsd->bhnts', q_t, k_t)
    scores = jnp.where(causal, scores, 0.0)
    o_intra = jnp.einsum('bhnts,bhnsv->bhntv', scores, v)
    k_end = k * jnp.exp(b_last - b)
    d_state = jnp.einsum('bhnsd,bhnsv->bhndv', k_end, v)
    decay = jnp.exp(b_last[:, :, :, 0, :])

    def step(state, inp):
        dec, ds = inp
        return dec[..., None] * state + ds, state

    s0 = jnp.zeros((B, H, DK, DV), jnp.float32)
    _, s_start = lax.scan(step, s0, (jnp.moveaxis(decay, 2, 0), jnp.moveaxis(d_state, 2, 0)))
    s_start = jnp.moveaxis(s_start, 0, 2)
    o_inter = jnp.einsum('bhntd,bhndv->bhntv', q_t, s_start)
    o = o_intra + o_inter
    return o.transpose(0, 2, 3, 1, 4).reshape(B, S, H, DV)


def hybrid_ab_mixer(x, w_in, lb, hg_norm_g, cv_w, cv_b, cv_ln_g, cv_ln_b, w_out):
    B, S, _ = x.shape
    h = x @ w_in
    q, f_pre, i_v, g, a, a_gate = jnp.split(
        h, [HG_WIDTH, 2 * HG_WIDTH, 3 * HG_WIDTH, 4 * HG_WIDTH, 4 * HG_WIDTH + CV_CH], axis=-1)

    lb = lb.astype(jnp.float32)
    f_pre = f_pre.astype(jnp.float32)
    f = lb + (1.0 - lb) * jax.nn.sigmoid(f_pre)
    k = (1.0 - lb) * jax.nn.sigmoid(-f_pre)
    log_f = jnp.log(f)
    heads = lambda t: t.reshape(B, S, HG_HEADS, -1)
    o = hgrn2_chunked(heads(q), heads(k), heads(i_v), heads(log_f))
    o = o * lax.rsqrt(jnp.mean(jnp.square(o), axis=-1, keepdims=True) + RMS_EPS)
    o = o.reshape(B, S, HG_HEADS * HG_DV) * hg_norm_g.astype(jnp.float32) * jax.nn.silu(g.astype(jnp.float32))

    u = a * jax.nn.sigmoid(a_gate)
    u = lax.conv_general_dilated(
        u, cv_w.astype(u.dtype), window_strides=(1,), padding=[(CV_K - 1, 0)],
        dimension_numbers=('NWC', 'WIO', 'NWC'), feature_group_count=CV_CH) + cv_b
    u = jax.nn.silu(layer_norm(u, cv_ln_g, cv_ln_b))

    y = jnp.concatenate([o.astype(x.dtype), u.astype(x.dtype)], axis=-1)
    return y @ w_out


def multiscale_pool_mixer(x, pool_w, pool_scale):
    B, S, _ = x.shape
    xg = x.astype(jnp.float32).reshape(B, S, POOL_GROUPS, POOL_CH)
    cs = jnp.cumsum(xg, axis=1)
    pos = jnp.arange(1, S + 1, dtype=jnp.float32)
    feats = []
    for gi, w in enumerate(POOL_WINDOWS):
        c = cs[:, :, gi]
        lagged = jnp.pad(c, ((0, 0), (w, 0), (0, 0)))[:, :S]
        mean = (c - lagged) / jnp.minimum(pos, float(w))[:, None]
        feats.append(mean - xg[:, :, gi])
    p = jnp.stack(feats, axis=2).astype(x.dtype)
    y = jnp.einsum('bsgc,gcd->bsgd', p, pool_w).reshape(B, S, D_MODEL)
    return y * pool_scale


def memory_cross_attention(x, mem_n, w_q, w_k, w_v, w_o):
    B, S, _ = x.shape
    q = (x @ w_q).reshape(B, S, XA_HEADS, XA_DH)
    k = (mem_n @ w_k).reshape(B, -1, XA_HEADS, XA_DH)
    v = (mem_n @ w_v).reshape(B, -1, XA_HEADS, XA_DH)
    s = jnp.einsum('bshd,bmhd->bhsm', q, k).astype(jnp.float32) * (XA_DH ** -0.5)
    p = jax.nn.softmax(s, axis=-1).astype(x.dtype)
    o = jnp.einsum('bhsm,bmhd->bshd', p, v).reshape(B, S, D_MODEL)
    return o @ w_o


def swiglu_ffn(x, w_gate, w_up, w_down):
    return (jax.nn.silu(x @ w_gate) * (x @ w_up)) @ w_down


def setup_inputs(seed: int = 0) -> dict:
    key = jax.random.key(seed)
    ks = jax.random.split(key, 32)
    f32 = jnp.float32
    nrm = lambda k, shape, scale: jax.random.normal(k, shape, f32) * scale
    gain = lambda k, shape: 1.0 + 0.02 * jax.random.normal(k, shape, f32)
    bias = lambda k, shape: 0.02 * jax.random.normal(k, shape, f32)
    d = D_MODEL
    return {
        "x": jax.random.normal(ks[0], (BATCH, SEQ, d), f32),
        "mem": jax.random.normal(ks[1], (BATCH, MEM_LEN, d), f32),
        "lb_param": nrm(ks[2], (DEPTH + 1, HG_WIDTH), 0.1),
        "mem_ln_g": gain(ks[3], (d,)),
        "mem_ln_b": bias(ks[4], (d,)),
        "ab_w_in": nrm(ks[5], (N_EVEN, d, IN_COLS), d ** -0.5),
        "hg_norm_g": gain(ks[6], (N_EVEN, HG_HEADS * HG_DV)),
        "cv_w": nrm(ks[7], (N_EVEN, CV_K, 1, CV_CH), CV_K ** -0.5),
        "cv_b": bias(ks[8], (N_EVEN, CV_CH)),
        "cv_ln_g": gain(ks[9], (N_EVEN, CV_CH)),
        "cv_ln_b": bias(ks[10], (N_EVEN, CV_CH)),
        "ab_w_out": nrm(ks[11], (N_EVEN, d, d), BETA * d ** -0.5),
        "pool_w": nrm(ks[12], (N_ODD, POOL_GROUPS, POOL_CH, POOL_CH), BETA * POOL_CH ** -0.5),
        "pool_scale": gain(ks[13], (N_ODD, d)),
        "ln_mix_g": gain(ks[14], (DEPTH, d)),
        "ln_mix_b": bias(ks[15], (DEPTH, d)),
        "xa_wq": nrm(ks[16], (DEPTH, d, d), d ** -0.5),
        "xa_wk": nrm(ks[17], (DEPTH, d, d), d ** -0.5),
        "xa_wv": nrm(ks[18], (DEPTH, d, d), d ** -0.5),
        "xa_wo": nrm(ks[19], (DEPTH, d, d), BETA * d ** -0.5),
        "ln_xa_g": gain(ks[20], (DEPTH, d)),
        "ln_xa_b": bias(ks[21], (DEPTH, d)),
        "ffn_wg": nrm(ks[22], (DEPTH, d, D_FF), d ** -0.5),
        "ffn_wu": nrm(ks[23], (DEPTH, d, D_FF), d ** -0.5),
        "ffn_wd": nrm(ks[24], (DEPTH, D_FF, d), BETA * D_FF ** -0.5),
        "ln_ffn_g": gain(ks[25], (DEPTH, d)),
        "ln_ffn_b": bias(ks[26], (DEPTH, d)),
    }


def reference(x, mem, lb_param, mem_ln_g, mem_ln_b, ab_w_in, hg_norm_g, cv_w, cv_b,
              cv_ln_g, cv_ln_b, ab_w_out, pool_w, pool_scale, ln_mix_g, ln_mix_b,
              xa_wq, xa_wk, xa_wv, xa_wo, ln_xa_g, ln_xa_b, ffn_wg, ffn_wu, ffn_wd,
              ln_ffn_g, ln_ffn_b):
    lb_all = jnp.cumsum(jax.nn.softmax(lb_param.astype(jnp.float32), axis=0), axis=0)
    mem_n = layer_norm(mem, mem_ln_g, mem_ln_b)
    for l in range(DEPTH):
        if l % 2 == 0:
            e = l // 2
            y = hybrid_ab_mixer(x, ab_w_in[e], lb_all[l], hg_norm_g[e], cv_w[e], cv_b[e],
                                cv_ln_g[e], cv_ln_b[e], ab_w_out[e])
        else:
            o = l // 2
            y = multiscale_pool_mixer(x, pool_w[o], pool_scale[o])
        x = layer_norm(ALPHA * x + y, ln_mix_g[l], ln_mix_b[l])
        y = memory_cross_attention(x, mem_n, xa_wq[l], xa_wk[l], xa_wv[l], xa_wo[l])
        x = layer_norm(ALPHA * x + y, ln_xa_g[l], ln_xa_b[l])
        y = swiglu_ffn(x, ffn_wg[l], ffn_wu[l], ffn_wd[l])
        x = layer_norm(ALPHA * x + y, ln_ffn_g[l], ln_ffn_b[l])
    return x
```

```python
import functools

import jax
import jax.numpy as jnp
from jax import lax
from jax.experimental import pallas as pl
from jax.experimental.pallas import tpu as pltpu

F32 = jnp.float32
BF16 = jnp.bfloat16

D_MODEL = 1024
DEPTH = 2

HG_HEADS = 4
HG_DK = 128
HG_DV = 128
HG_WIDTH = HG_HEADS * HG_DK
HG_SUB = 32
HG_BLOCK = 128
HG_NSUB = HG_BLOCK // HG_SUB

CV_CH = D_MODEL - HG_HEADS * HG_DV
CV_K = 31
CV_HALO = 32
CV_ROWS = 64

POOL_WINDOWS = (2, 4, 8, 16)
POOL_CH = D_MODEL // len(POOL_WINDOWS)
POOL_HALO = 16

MEM_LEN = 256
XA_HEADS = 4
XA_DH = D_MODEL // XA_HEADS
D_FF = -(-(8 * D_MODEL) // (3 * 256)) * 256

ALPHA = (2 * DEPTH) ** 0.25
LN_EPS = 1e-5
RMS_EPS = 1e-6

VMEM_LIMIT_BYTES = 56 * 1024 * 1024

MIX_TILE = 512
XA_TILE = 512
FFN_TILE = 512
FFN_CHUNKS = (768, 768, 768, 512)


def _layer_norm(x, g, b):
    mu = jnp.mean(x, axis=-1, keepdims=True)
    xc = x - mu
    var = jnp.mean(xc * xc, axis=-1, keepdims=True)
    return xc * lax.rsqrt(var + LN_EPS) * g + b


def _dot(a, b):
    return jnp.dot(a, b, preferred_element_type=F32)


def _dot_nt(a, b):
    return lax.dot_general(a, b, (((1,), (1,)), ((), ())), preferred_element_type=F32)


def _dot_tn(a, b):
    return lax.dot_general(a, b, (((0,), (0,)), ((), ())), preferred_element_type=F32)


def _full(shape):
    return pl.BlockSpec(shape, lambda *_: (0,) * len(shape))


def _params(*semantics):
    return pltpu.CompilerParams(dimension_semantics=semantics, vmem_limit_bytes=VMEM_LIMIT_BYTES)


def _mem_kv_kernel(mem_ref, g_ref, b_ref, wk_ref, wv_ref, k_ref, v_ref):
    mem_n = _layer_norm(mem_ref[0], g_ref[...], b_ref[...]).astype(BF16)
    for l in range(DEPTH):
        k_ref[l, 0] = _dot(mem_n, wk_ref[l]).astype(BF16)
        v_ref[l, 0] = _dot(mem_n, wv_ref[l]).astype(BF16)


def _mem_kv(mem, g, b, wk, wv):
    nb = mem.shape[0]
    out = jax.ShapeDtypeStruct((DEPTH, nb, MEM_LEN, D_MODEL), BF16)
    return pl.pallas_call(
        _mem_kv_kernel,
        grid=(nb,),
        in_specs=[
            pl.BlockSpec((1, MEM_LEN, D_MODEL), lambda i: (i, 0, 0)),
            _full((1, D_MODEL)), _full((1, D_MODEL)),
            _full((DEPTH, D_MODEL, D_MODEL)), _full((DEPTH, D_MODEL, D_MODEL)),
        ],
        out_specs=[pl.BlockSpec((DEPTH, 1, MEM_LEN, D_MODEL), lambda i: (0, i, 0, 0))] * 2,
        out_shape=[out, out],
        compiler_params=_params("arbitrary"),
        name="mem_kv",
    )(mem, g, b, wk, wv)


def _hgrn_block(h_ref, r0, lb, tri, diag_masks, state_ref):
    rows = pl.ds(r0, HG_BLOCK)
    q = h_ref[rows, 0:HG_WIDTH]
    f_pre = h_ref[rows, HG_WIDTH:2 * HG_WIDTH]
    v = h_ref[rows, 2 * HG_WIDTH:3 * HG_WIDTH].astype(BF16)

    f = lb + (1.0 - lb) * jax.nn.sigmoid(f_pre)
    k = (1.0 - lb) * jax.nn.sigmoid(-f_pre)
    log_f = jnp.log(f)

    hi = log_f.astype(BF16)
    r1 = log_f - hi.astype(F32)
    mid = r1.astype(BF16)
    lo = (r1 - mid.astype(F32)).astype(BF16)
    b = _dot(tri, hi) + _dot(tri, mid) + _dot(tri, lo)

    q0 = q * jnp.exp(b)
    k0 = k * jnp.exp(-b)

    tot = [b[HG_SUB * (c + 1) - 1:HG_SUB * (c + 1), :] for c in range(HG_NSUB)]
    start = [jnp.zeros_like(tot[0])]
    for c in range(HG_NSUB):
        start.append(start[c] + tot[c])

    def span(c_from, c_to):
        return jnp.broadcast_to(jnp.exp(start[c_to] - start[c_from]), (HG_SUB, HG_WIDTH))

    def chunk(a, c):
        return a[HG_SUB * c:HG_SUB * (c + 1), :]

    zeros = jnp.zeros((HG_SUB, HG_WIDTH), F32)
    k_to = [
        jnp.concatenate([chunk(k0, c) * span(c, j) if c < j else zeros for c in range(HG_NSUB)], axis=0)
        for j in range(1, HG_NSUB)
    ]
    k_end = jnp.concatenate([chunk(k0, c) * span(c, HG_NSUB) for c in range(HG_NSUB)], axis=0)
    q_blk = jnp.concatenate([chunk(q0, c) * span(0, c) for c in range(HG_NSUB)], axis=0)
    decay = jnp.exp(start[HG_NSUB])

    q0 = q0.astype(BF16)
    q_blk = q_blk.astype(BF16)
    k_end = k_end.astype(BF16)
    k_all = [k0.astype(BF16)] + [kj.astype(BF16) for kj in k_to]

    outs = []
    for h in range(HG_HEADS):
        cols = slice(HG_DK * h, HG_DK * (h + 1))
        keys = jnp.concatenate([kj[:, cols] for kj in k_all], axis=0)
        s = _dot_nt(q0[:, cols], keys)
        p_rows = []
        for j in range(HG_NSUB):
            rj = slice(HG_SUB * j, HG_SUB * (j + 1))
            same_chunk = s[rj, 0:HG_BLOCK]
            earlier = s[rj, HG_BLOCK * j:HG_BLOCK * (j + 1)] if j else jnp.zeros((HG_SUB, HG_BLOCK), F32)
            p_rows.append(jnp.where(diag_masks[j], same_chunk, earlier))
        p = jnp.concatenate(p_rows, axis=0).astype(BF16)
        vh = v[:, cols]
        state_t = state_ref[h]
        o = _dot(p, vh) + _dot_nt(q_blk[:, cols], state_t.astype(BF16))
        state_ref[h] = decay[:, cols] * state_t + _dot_tn(vh, k_end[:, cols])
        outs.append(o * lax.rsqrt(jnp.mean(o * o, axis=-1, keepdims=True) + RMS_EPS))
    return jnp.concatenate(outs, axis=-1)


def _mix_even_kernel(x_ref, w_in_ref, lbp_ref, hgn_ref, cvw_ref, cvb_ref, cvg_ref, cvbeta_ref, w_out_ref,
                     lng_ref, lnb_ref, o_ref, h_ref, y_ref, u_ref, state_ref, *, layer, tile):
    @pl.when(pl.program_id(1) == 0)
    def _():
        state_ref[...] = jnp.zeros_like(state_ref)
        u_ref[0:CV_HALO, :] = jnp.zeros((CV_HALO, CV_CH), F32)

    x = x_ref[0]
    h_ref[...] = _dot(x.astype(BF16), w_in_ref[...])

    rows = [lbp_ref[i:i + 1, :] for i in range(DEPTH + 1)]
    top = functools.reduce(jnp.maximum, rows)
    e = [jnp.exp(r - top) for r in rows]
    lb = sum(e[:layer + 1]) / sum(e)

    ri = lax.broadcasted_iota(jnp.int32, (HG_BLOCK, HG_BLOCK), 0)
    ci = lax.broadcasted_iota(jnp.int32, (HG_BLOCK, HG_BLOCK), 1)
    tri = ((ci <= ri) & (ri // HG_SUB == ci // HG_SUB)).astype(BF16)
    r32 = lax.broadcasted_iota(jnp.int32, (HG_SUB, HG_BLOCK), 0)
    c32 = lax.broadcasted_iota(jnp.int32, (HG_SUB, HG_BLOCK), 1)
    diag_masks = [(c32 >= HG_SUB * j) & (c32 - HG_SUB * j <= r32) for j in range(HG_NSUB)]

    hgn = hgn_ref[...]
    for blk in range(tile // HG_BLOCK):
        r0 = blk * HG_BLOCK
        o = _hgrn_block(h_ref, r0, lb, tri, diag_masks, state_ref)
        g = h_ref[pl.ds(r0, HG_BLOCK), 3 * HG_WIDTH:4 * HG_WIDTH]
        y_ref[pl.ds(r0, HG_BLOCK), 0:HG_WIDTH] = (o * hgn * (g * jax.nn.sigmoid(g))).astype(BF16)

    a0 = 4 * HG_WIDTH
    u_ref[CV_HALO:CV_HALO + tile, :] = h_ref[:, a0:a0 + CV_CH] * jax.nn.sigmoid(h_ref[:, a0 + CV_CH:a0 + 2 * CV_CH])
    cvw = cvw_ref[...]
    cvb, cvg, cvbeta = cvb_ref[...], cvg_ref[...], cvbeta_ref[...]
    for rb in range(tile // CV_ROWS):
        base = CV_HALO + rb * CV_ROWS - (CV_K - 1)
        acc = jnp.broadcast_to(cvb, (CV_ROWS, CV_CH))
        for t in range(CV_K):
            acc = acc + cvw[t:t + 1, :] * u_ref[base + t:base + t + CV_ROWS, :]
        c = _layer_norm(acc, cvg, cvbeta)
        y_ref[rb * CV_ROWS:(rb + 1) * CV_ROWS, HG_WIDTH:D_MODEL] = (c * jax.nn.sigmoid(c)).astype(BF16)
    u_ref[0:CV_HALO, :] = u_ref[tile:tile + CV_HALO, :]

    y = _dot(y_ref[...], w_out_ref[...])
    o_ref[0] = _layer_norm(ALPHA * x + y, lng_ref[...], lnb_ref[...])


def _mix_even(x, w_in, lb_param, hg_norm_g, cv_w, cv_b, cv_ln_g, cv_ln_b, w_out, ln_g, ln_b, *, layer):
    nb, seq, _ = x.shape
    tile = min(MIX_TILE, seq)
    in_cols = w_in.shape[1]
    row = lambda n: _full((1, n))
    return pl.pallas_call(
        functools.partial(_mix_even_kernel, layer=layer, tile=tile),
        grid=(nb, seq // tile),
        in_specs=[
            pl.BlockSpec((1, tile, D_MODEL), lambda b, s: (b, s, 0)),
            _full((D_MODEL, in_cols)),
            _full((DEPTH + 1, HG_WIDTH)),
            row(HG_WIDTH),
            _full((CV_K, CV_CH)), row(CV_CH), row(CV_CH), row(CV_CH),
            _full((D_MODEL, D_MODEL)),
            row(D_MODEL), row(D_MODEL),
        ],
        out_specs=pl.BlockSpec((1, tile, D_MODEL), lambda b, s: (b, s, 0)),
        out_shape=jax.ShapeDtypeStruct(x.shape, F32),
        scratch_shapes=[
            pltpu.VMEM((tile, in_cols), F32),
            pltpu.VMEM((tile, D_MODEL), BF16),
            pltpu.VMEM((CV_HALO + tile, CV_CH), F32),
            pltpu.VMEM((HG_HEADS, HG_DV, HG_DK), F32),
        ],
        compiler_params=_params("arbitrary", "arbitrary"),
        name=f"mix_even_{layer}",
    )(x, w_in, lb_param, hg_norm_g, cv_w, cv_b, cv_ln_g, cv_ln_b, w_out, ln_g, ln_b)


def _mix_pool_kernel(x_ref, pw_ref, ps_ref, lng_ref, lnb_ref, o_ref, xh_ref, *, tile):
    s = pl.program_id(1)

    @pl.when(s == 0)
    def _():
        xh_ref[0:POOL_HALO, :] = jnp.zeros((POOL_HALO, D_MODEL), F32)

    x = x_ref[0]
    xh_ref[POOL_HALO:POOL_HALO + tile, :] = x
    pos = (s * tile + 1 + lax.broadcasted_iota(jnp.int32, (tile, POOL_CH), 0)).astype(F32)
    ys = []
    for gi, w in enumerate(POOL_WINDOWS):
        cols = slice(POOL_CH * gi, POOL_CH * (gi + 1))
        acc = x[:, cols]
        for j in range(1, w):
            acc = acc + xh_ref[POOL_HALO - j:POOL_HALO - j + tile, cols]
        feat = acc / jnp.minimum(pos, float(w)) - x[:, cols]
        ys.append(_dot(feat.astype(BF16), pw_ref[gi]))
    xh_ref[0:POOL_HALO, :] = xh_ref[tile:tile + POOL_HALO, :]
    y = jnp.concatenate(ys, axis=-1) * ps_ref[...]
    o_ref[0] = _layer_norm(ALPHA * x + y, lng_ref[...], lnb_ref[...])


def _mix_pool(x, pool_w, pool_scale, ln_g, ln_b):
    nb, seq, _ = x.shape
    tile = min(MIX_TILE, seq)
    return pl.pallas_call(
        functools.partial(_mix_pool_kernel, tile=tile),
        grid=(nb, seq // tile),
        in_specs=[
            pl.BlockSpec((1, tile, D_MODEL), lambda b, s: (b, s, 0)),
            _full((len(POOL_WINDOWS), POOL_CH, POOL_CH)),
            _full((1, D_MODEL)), _full((1, D_MODEL)), _full((1, D_MODEL)),
        ],
        out_specs=pl.BlockSpec((1, tile, D_MODEL), lambda b, s: (b, s, 0)),
        out_shape=jax.ShapeDtypeStruct(x.shape, F32),
        scratch_shapes=[pltpu.VMEM((POOL_HALO + tile, D_MODEL), F32)],
        compiler_params=_params("arbitrary", "arbitrary"),
        name="mix_pool",
    )(x, pool_w, pool_scale, ln_g, ln_b)


def _xattn_kernel(x_ref, wq_ref, k_ref, v_ref, wo_ref, lng_ref, lnb_ref, o_ref):
    x = x_ref[0]
    q = _dot(x.astype(BF16), wq_ref[...]).astype(BF16)
    heads = []
    for h in range(XA_HEADS):
        cols = slice(XA_DH * h, XA_DH * (h + 1))
        s = _dot_nt(q[:, cols], k_ref[0, 0, :, cols]) * (XA_DH ** -0.5)
        p = jnp.exp(s - jnp.max(s, axis=-1, keepdims=True))
        p = p / jnp.sum(p, axis=-1, keepdims=True)
        heads.append(_dot(p.astype(BF16), v_ref[0, 0, :, cols]).astype(BF16))
    y = _dot(jnp.concatenate(heads, axis=-1), wo_ref[...])
    o_ref[0] = _layer_norm(ALPHA * x + y, lng_ref[...], lnb_ref[...])


def _xattn(x, wq, k_all, v_all, wo, ln_g, ln_b, *, layer):
    nb, seq, _ = x.shape
    tile = min(XA_TILE, seq)
    kv_spec = pl.BlockSpec((1, 1, MEM_LEN, D_MODEL), lambda b, s: (layer, b, 0, 0))
    return pl.pallas_call(
        _xattn_kernel,
        grid=(nb, seq // tile),
        in_specs=[
            pl.BlockSpec((1, tile, D_MODEL), lambda b, s: (b, s, 0)),
            _full((D_MODEL, D_MODEL)),
            kv_spec, kv_spec,
            _full((D_MODEL, D_MODEL)),
            _full((1, D_MODEL)), _full((1, D_MODEL)),
        ],
        out_specs=pl.BlockSpec((1, tile, D_MODEL), lambda b, s: (b, s, 0)),
        out_shape=jax.ShapeDtypeStruct(x.shape, F32),
        compiler_params=_params("arbitrary", "arbitrary"),
        name=f"xattn_{layer}",
    )(x, wq, k_all, v_all, wo, ln_g, ln_b)


def _ffn_kernel(x_ref, wg_ref, wu_ref, wd_ref, lng_ref, lnb_ref, o_ref):
    x = x_ref[...]
    xb = x.astype(BF16)
    y = None
    c0 = 0
    for width in FFN_CHUNKS:
        g = _dot(xb, wg_ref[:, c0:c0 + width])
        u = _dot(xb, wu_ref[:, c0:c0 + width])
        part = _dot((g * jax.nn.sigmoid(g) * u).astype(BF16), wd_ref[c0:c0 + width, :])
        y = part if y is None else y + part
        c0 += width
    o_ref[...] = _layer_norm(ALPHA * x + y, lng_ref[...], lnb_ref[...])


def _ffn(x, wg, wu, wd, ln_g, ln_b, *, layer):
    n = x.shape[0]
    tile = min(FFN_TILE, n)
    assert sum(FFN_CHUNKS) == D_FF
    return pl.pallas_call(
        _ffn_kernel,
        grid=(n // tile,),
        in_specs=[
            pl.BlockSpec((tile, D_MODEL), lambda i: (i, 0)),
            _full((D_MODEL, D_FF)), _full((D_MODEL, D_FF)), _full((D_FF, D_MODEL)),
            _full((1, D_MODEL)), _full((1, D_MODEL)),
        ],
        out_specs=pl.BlockSpec((tile, D_MODEL), lambda i: (i, 0)),
        out_shape=jax.ShapeDtypeStruct(x.shape, F32),
        compiler_params=_params("arbitrary"),
        name=f"ffn_{layer}",
    )(x, wg, wu, wd, ln_g, ln_b)


def kernel(x, mem, lb_param, mem_ln_g, mem_ln_b, ab_w_in, hg_norm_g, cv_w, cv_b, cv_ln_g, cv_ln_b, ab_w_out,
           pool_w, pool_scale, ln_mix_g, ln_mix_b, xa_wq, xa_wk, xa_wv, xa_wo, ln_xa_g, ln_xa_b,
           ffn_wg, ffn_wu, ffn_wd, ln_ffn_g, ln_ffn_b):
    nb, seq, d = x.shape
    assert d == D_MODEL and seq % min(MIX_TILE, seq) == 0 and min(MIX_TILE, seq) % HG_BLOCK == 0
    bf = lambda w: w.astype(BF16)
    row = lambda v: v.reshape(1, -1)

    k_all, v_all = _mem_kv(mem, row(mem_ln_g), row(mem_ln_b), bf(xa_wk), bf(xa_wv))
    for l in range(DEPTH):
        if l % 2 == 0:
            e = l // 2
            x = _mix_even(x, bf(ab_w_in[e]), lb_param, row(hg_norm_g[e]), cv_w[e].reshape(CV_K, CV_CH),
                          row(cv_b[e]), row(cv_ln_g[e]), row(cv_ln_b[e]), bf(ab_w_out[e]),
                          row(ln_mix_g[l]), row(ln_mix_b[l]), layer=l)
        else:
            o = l // 2
            x = _mix_pool(x, bf(pool_w[o]), row(pool_scale[o]), row(ln_mix_g[l]), row(ln_mix_b[l]))
        x = _xattn(x, bf(xa_wq[l]), k_all, v_all, bf(xa_wo[l]), row(ln_xa_g[l]), row(ln_xa_b[l]), layer=l)
        x = _ffn(x.reshape(nb * seq, d), bf(ffn_wg[l]), bf(ffn_wu[l]), bf(ffn_wd[l]),
                 row(ln_ffn_g[l]), row(ln_ffn_b[l]), layer=l).reshape(nb, seq, d)
    return x
```
